```python
import math
import jax, jax.numpy as jnp
from jax import lax
import numpy as np

D_MODEL = 2048
BATCH = 4
SEQ = 2048
DEPTH = 1
DEC_BATCH = 128
DEC_SEQ = 8
PAST_LEN = 16384
PAGE_SIZE = 128

D_MIX = D_MODEL
DK = 128
DV = 128
H_A = (D_MIX // 2) // DV
W_A = H_A * DV
C_B = D_MIX - W_A
N_CONV_GROUPS = C_B // 128
CONV_W = 31
D_FF = 4 * D_MODEL
CHUNK = 64
EPS = 1e-6
D_IN = 4 * W_A + 2 * C_B

kernel_name = "hybrid_hgrn2_conformer_conv_decode_step"


def rmsnorm(x, g):
    x32 = x.astype(jnp.float32)
    y = x32 * lax.rsqrt(jnp.mean(x32 * x32, axis=-1, keepdims=True) + EPS)
    return y.astype(x.dtype) * g


def hgrn2_chunked(q, k, v, logf, s0, chunk):
    B, L, H, _ = q.shape
    n = L // chunk

    def to_chunks(t):
        return t.reshape(B, n, chunk, H, t.shape[-1]).transpose(1, 0, 3, 2, 4)

    mask = jnp.tril(jnp.ones((chunk, chunk), dtype=bool))[None, None, :, :, None]

    def step(S, inp):
        qc, kc, vc, gc = inp
        b = jnp.cumsum(gc, axis=2)
        diff = b[:, :, :, None, :] - b[:, :, None, :, :]
        decay = jnp.exp(jnp.where(mask, diff, -jnp.inf))
        attn = jnp.einsum('bhtk,bhsk,bhtsk->bhts', qc, kc, decay)
        o = (jnp.einsum('bhts,bhsv->bhtv', attn, vc)
             + jnp.einsum('bhtk,bhkv->bhtv', qc * jnp.exp(b), S))
        b_last = b[:, :, -1:, :]
        S_new = (jnp.exp(b_last[:, :, 0, :])[..., None] * S
                 + jnp.einsum('bhsk,bhsv->bhkv', kc * jnp.exp(b_last - b), vc))
        return S_new, o

    S_fin, o = lax.scan(step, s0, (to_chunks(q), to_chunks(k), to_chunks(v), to_chunks(logf)))
    o = o.transpose(1, 0, 3, 2, 4).reshape(B, L, H, v.shape[-1])
    return o, S_fin


def mixer(xn, S0, buf, lb, w_in, hgrn_norm_g, conv_w, conv_b, conv_ln_g, conv_ln_b, w_out):
    B, L, _ = xn.shape
    z = xn @ w_in
    q, fl, i, g, a, gate = jnp.split(
        z, [W_A, 2 * W_A, 3 * W_A, 4 * W_A, 4 * W_A + C_B], axis=-1)

    f32 = jnp.float32
    qh = (jax.nn.silu(q).astype(f32) * (DK ** -0.5)).reshape(B, L, H_A, DK)
    f = lb + (1.0 - lb) * jax.nn.sigmoid(fl.astype(f32))
    logf = jnp.log(f).reshape(B, L, H_A, DK)
    kh = (1.0 - f).reshape(B, L, H_A, DK)
    vh = i.astype(f32).reshape(B, L, H_A, DV)
    chunk = math.gcd(L, CHUNK)
    o, S_fin = hgrn2_chunked(qh, kh, vh, logf, S0.astype(f32), chunk)
    o = o * lax.rsqrt(jnp.mean(o * o, axis=-1, keepdims=True) + EPS)
    o = o.reshape(B, L, W_A).astype(xn.dtype) * hgrn_norm_g * jax.nn.silu(g)

    u = a * jax.nn.sigmoid(gate)
    full = jnp.concatenate([buf.astype(u.dtype), u], axis=1)
    c = lax.conv_general_dilated(
        full, conv_w[:, None, :].astype(u.dtype), window_strides=(1,), padding='VALID',
        dimension_numbers=('NWC', 'WIO', 'NWC'), feature_group_count=C_B) + conv_b
    new_buf = full[:, -(CONV_W - 1):, :]
    c32 = c.astype(f32)
    mu = jnp.mean(c32, axis=-1, keepdims=True)
    var = jnp.mean(jnp.square(c32 - mu), axis=-1, keepdims=True)
    c = ((c32 - mu) * lax.rsqrt(var + EPS)).astype(c.dtype) * conv_ln_g + conv_ln_b
    c = jax.nn.silu(c)

    out = jnp.concatenate([o, c], axis=-1) @ w_out
    return out, S_fin.astype(S0.dtype), new_buf


def mlp(xn, w_up, w_down):
    h = jnp.square(jax.nn.relu(xn @ w_up))
    return h @ w_down


def setup_inputs(seed: int = 0) -> dict:
    key = jax.random.key(seed)
    ks = jax.random.split(key, 20)
    f32 = jnp.float32
    nrm = lambda k, s, sc: jax.random.normal(k, s, f32) * sc
    return {
        'x_prompt': nrm(ks[0], (BATCH, SEQ, D_MODEL), 1.0),
        'x_sample': nrm(ks[1], (DEC_BATCH, DEC_SEQ, D_MODEL), 1.0),
        'state_hgrn': nrm(ks[2], (DEPTH, DEC_BATCH, H_A, DK, DV), 0.3),
        'state_conv': nrm(ks[3], (DEPTH, DEC_BATCH, CONV_W - 1, C_B), 0.5),
        'lower_bounds': nrm(ks[4], (DEPTH + 1, H_A * DK), 0.5),
        'norm1_g': 1.0 + nrm(ks[5], (DEPTH, D_MODEL), 0.02),
        'w_in': nrm(ks[6], (DEPTH, D_MODEL, D_IN), D_MODEL ** -0.5),
        'hgrn_norm_g': 1.0 + nrm(ks[7], (DEPTH, W_A), 0.02),
        'conv_w': nrm(ks[8], (DEPTH, CONV_W, C_B), CONV_W ** -0.5),
        'conv_b': nrm(ks[9], (DEPTH, C_B), 0.02),
        'conv_ln_g': 1.0 + nrm(ks[10], (DEPTH, C_B), 0.02),
        'conv_ln_b': nrm(ks[11], (DEPTH, C_B), 0.02),
        'w_out': nrm(ks[12], (DEPTH, D_MIX, D_MODEL), D_MIX ** -0.5),
        'norm2_g': 1.0 + nrm(ks[13], (DEPTH, D_MODEL), 0.02),
        'w_up': nrm(ks[14], (DEPTH, D_MODEL, D_FF), D_MODEL ** -0.5),
        'w_down': nrm(ks[15], (DEPTH, D_FF, D_MODEL), D_FF ** -0.5),
        'final_norm_g': 1.0 + nrm(ks[16], (D_MODEL,), 0.02),
    }


def reference(x_prompt, x_sample, state_hgrn, state_conv, lower_bounds, norm1_g, w_in,
              hgrn_norm_g, conv_w, conv_b, conv_ln_g, conv_ln_b, w_out, norm2_g,
              w_up, w_down, final_norm_g):
    lb_all = jnp.cumsum(jax.nn.softmax(lower_bounds.astype(jnp.float32), axis=0), axis=0)
    xp, xs = x_prompt, x_sample
    Bp = x_prompt.shape[0]
    hp_list, cp_list, hs_list, cs_list = [], [], [], []
    for l in range(DEPTH):
        lb = lb_all[l]
        lw = (w_in[l], hgrn_norm_g[l], conv_w[l], conv_b[l], conv_ln_g[l], conv_ln_b[l], w_out[l])
        S0p = jnp.zeros((Bp, H_A, DK, DV), state_hgrn.dtype)
        buf0p = jnp.zeros((Bp, CONV_W - 1, C_B), xp.dtype)
        mp, Sp, bp = mixer(rmsnorm(xp, norm1_g[l]), S0p, buf0p, lb, *lw)
        xp = xp + mp
        xp = xp + mlp(rmsnorm(xp, norm2_g[l]), w_up[l], w_down[l])
        ms, Ss, bs = mixer(rmsnorm(xs, norm1_g[l]), state_hgrn[l], state_conv[l], lb, *lw)
        xs = xs + ms
        xs = xs + mlp(rmsnorm(xs, norm2_g[l]), w_up[l], w_down[l])
        hp_list.append(Sp); cp_list.append(bp); hs_list.append(Ss); cs_list.append(bs)
    y_prompt = rmsnorm(xp, final_norm_g)
    y_sample = rmsnorm(xs, final_norm_g)
    new_hgrn_prompt = jnp.stack(hp_list, axis=0)
    new_conv_prompt = jnp.stack(cp_list, axis=0)
    new_hgrn_sample = jnp.stack(hs_list, axis=0)
    new_conv_sample = jnp.stack(cs_list, axis=0)
    return (y_prompt, y_sample, new_hgrn_prompt, new_conv_prompt, new_hgrn_sample, new_conv_sample)
```

```python
import functools

import numpy as np
import jax
import jax.numpy as jnp
from jax import lax
from jax.experimental import pallas as pl
from jax.experimental.pallas import tpu as pltpu

F32 = jnp.float32
BF16 = jnp.bfloat16

D_MODEL = 2048
DK = 128
DV = 128
H_A = 8
W_A = H_A * DV
C_B = D_MODEL - W_A
CONV_W = 31
HIST = CONV_W - 1
D_FF = 4 * D_MODEL
EPS = 1e-6
SLAB = 64
HIST_PAD = 32
VMEM_LIMIT = 56 * 1024 * 1024


def _sigmoid(x):
    return 1.0 / (1.0 + jnp.exp(-x))


def _dot(a, b):
    return jnp.dot(a, b, preferred_element_type=F32)


def _dot_nt(a, b):
    return lax.dot_general(a, b, (((1,), (1,)), ((), ())), preferred_element_type=F32)


def _params(sem):
    return pltpu.CompilerParams(dimension_semantics=sem, vmem_limit_bytes=VMEM_LIMIT)


def _inproj_kernel(x_ref, g1_ref, lbnd_ref, wq, wf, wi, wg, wa, wgt,
                   q_o, lf_o, k_o, v_o, sg_o, u_o, xn_scr, *, layer):
    @pl.when(pl.program_id(1) == 0)
    def _():
        x = x_ref[...]
        ms = jnp.mean(x * x, axis=-1, keepdims=True)
        xn_scr[...] = (x * lax.rsqrt(ms + EPS) * g1_ref[...]).astype(BF16)

    xn = xn_scr[...]
    lbn = lbnd_ref[...]
    e = jnp.exp(lbn - jnp.max(lbn, axis=0, keepdims=True))
    lb = jnp.sum(e[0:layer + 1, :], axis=0, keepdims=True) / jnp.sum(e, axis=0, keepdims=True)

    zq = _dot(xn, wq[...])
    q_o[...] = zq * _sigmoid(zq) * (DK ** -0.5)
    zf = _dot(xn, wf[...])
    f = lb + (1.0 - lb) * _sigmoid(zf)
    lf_o[...] = jnp.log(f)
    k_o[...] = 1.0 - f
    v_o[...] = _dot(xn, wi[...])
    zg = _dot(xn, wg[...])
    sg_o[...] = zg * _sigmoid(zg)
    za = _dot(xn, wa[...])
    zt = _dot(xn, wgt[...])
    u_o[...] = za * _sigmoid(zt)


def _in_proj(x2d, g1, lower_bounds, w_in_bf, layer, tm=512, tn=256):
    T = x2d.shape[0]
    nseg = W_A // tn
    wspecs = [pl.BlockSpec((D_MODEL, tn), functools.partial(lambda i, j, s: (0, s * nseg + j), s=s))
              for s in range(6)]
    ospec = pl.BlockSpec((tm, tn), lambda i, j: (i, j))
    out_sds = jax.ShapeDtypeStruct((T, W_A), F32)
    return pl.pallas_call(
        functools.partial(_inproj_kernel, layer=layer),
        grid=(T // tm, nseg),
        in_specs=[pl.BlockSpec((tm, D_MODEL), lambda i, j: (i, 0)),
                  pl.BlockSpec((1, D_MODEL), lambda i, j: (0, 0)),
                  pl.BlockSpec((lower_bounds.shape[0], tn), lambda i, j: (0, j))] + wspecs,
        out_specs=[ospec] * 6,
        out_shape=[out_sds] * 6,
        scratch_shapes=[pltpu.VMEM((tm, D_MODEL), BF16)],
        compiler_params=_params(("parallel", "arbitrary")),
        name="in_proj",
    )(x2d, g1, lower_bounds, *([w_in_bf] * 6))


def _hier_consts(seq_len, n_seq):
    n = seq_len * n_seq
    halves = []
    m = 1
    while m < seq_len:
        halves.append(m)
        m *= 2
    nblk = len(halves) + 2
    mall = np.zeros((nblk * n, n), np.float32)
    masks = np.zeros((len(halves) + 1, n, n), np.float32)
    masks[0] = np.eye(n)
    for t in range(n):
        s0 = (t // seq_len) * seq_len
        for li, m in enumerate(halves):
            b0 = (t // (2 * m)) * 2 * m
            bound = b0 + m - 1
            if t - b0 >= m:
                mall[li * n + t, bound + 1:t + 1] = 1.0
                masks[li + 1, t, b0:b0 + m] = 1.0
            else:
                mall[li * n + t, t + 1:bound + 1] = 1.0
        mall[len(halves) * n + t, s0:t + 1] = 1.0
        mall[(len(halves) + 1) * n + t, t + 1:s0 + seq_len] = 1.0
    return jnp.asarray(mall, BF16), jnp.asarray(masks, F32), len(halves)


def _intra(lf, q, k, mall_ref, masks_ref, nlev):
    hi = lf.astype(BF16)
    lo = (lf - hi.astype(F32)).astype(BF16)
    mall = mall_ref[...]
    e_all = jnp.exp(_dot(mall, hi) + _dot(mall, lo))
    a = _dot_nt(q.astype(BF16), k.astype(BF16)) * masks_ref[0]
    for l in range(nlev):
        el = e_all[l * SLAB:(l + 1) * SLAB]
        a = a + _dot_nt((q * el).astype(BF16), (k * el).astype(BF16)) * masks_ref[l + 1]
    e_pre = e_all[nlev * SLAB:(nlev + 1) * SLAB]
    e_suf = e_all[(nlev + 1) * SLAB:(nlev + 2) * SLAB]
    return a, e_pre, e_suf


def _head_out(o, sg, gn):
    o = o * lax.rsqrt(jnp.mean(o * o, axis=-1, keepdims=True) + EPS)
    return (o * gn * sg).astype(BF16)


def _hgrn_prompt_kernel(lf_ref, q_ref, k_ref, v_ref, sg_ref, gn_ref, mall_ref, masks_ref,
                        o_ref, s_ref, st_scr, *, nlev, n_chunks):
    t = pl.program_id(2)

    @pl.when(t == 0)
    def _():
        st_scr[...] = jnp.zeros_like(st_scr)

    gn = gn_ref[...]

    def chunk(c, carry):
        rows = pl.ds(pl.multiple_of(c * SLAB, SLAB), SLAB)
        q = q_ref[rows, :]
        k = k_ref[rows, :]
        v = v_ref[rows, :]
        a, e_pre, e_suf = _intra(lf_ref[rows, :], q, k, mall_ref, masks_ref, nlev)
        st = st_scr[...]
        o = _dot(a.astype(BF16), v.astype(BF16)) + _dot_nt((q * e_pre).astype(BF16), st.astype(BF16))
        st_scr[...] = e_pre[SLAB - 1:SLAB, :] * st + _dot(v.T.astype(BF16), (k * e_suf).astype(BF16))
        o_ref[rows, :] = _head_out(o, sg_ref[rows, :], gn)
        return carry

    lax.fori_loop(0, n_chunks, chunk, 0)

    @pl.when(t == pl.num_programs(2) - 1)
    def _():
        s_ref[0, 0] = st_scr[...].T


def _hgrn_prompt(lf, q, k, v, sg, gn, batch, seq, lb_rows=512):
    mall, masks, nlev = _hier_consts(SLAB, 1)
    nt = seq // lb_rows
    spec = pl.BlockSpec((lb_rows, DK), lambda b, h, t: (b * nt + t, h))
    return pl.pallas_call(
        functools.partial(_hgrn_prompt_kernel, nlev=nlev, n_chunks=lb_rows // SLAB),
        grid=(batch, H_A, nt),
        in_specs=[spec] * 5 + [pl.BlockSpec((1, DK), lambda b, h, t: (0, h)),
                               pl.BlockSpec(mall.shape, lambda b, h, t: (0, 0)),
                               pl.BlockSpec(masks.shape, lambda b, h, t: (0, 0, 0))],
        out_specs=[spec, pl.BlockSpec((1, 1, DK, DV), lambda b, h, t: (b, h, 0, 0))],
        out_shape=[jax.ShapeDtypeStruct((batch * seq, W_A), BF16),
                   jax.ShapeDtypeStruct((batch, H_A, DK, DV), F32)],
        scratch_shapes=[pltpu.VMEM((DV, DK), F32)],
        compiler_params=_params(("parallel", "parallel", "arbitrary")),
        name="hgrn_prompt",
    )(lf, q, k, v, sg, gn, mall, masks)


def _hgrn_sample_kernel(lf_ref, q_ref, k_ref, v_ref, sg_ref, gn_ref, s0_ref, mall_ref, masks_ref,
                        o_ref, s_ref, *, nlev, seq, n_seq):
    row_seq = lax.broadcasted_iota(jnp.int32, (SLAB, DK), 0) // seq
    for h in range(H_A):
        cols = slice(h * DK, (h + 1) * DK)
        q = q_ref[:, cols]
        k = k_ref[:, cols]
        v = v_ref[:, cols]
        a, e_pre, e_suf = _intra(lf_ref[:, cols], q, k, mall_ref, masks_ref, nlev)
        o = _dot(a.astype(BF16), v.astype(BF16))
        qe = (q * e_pre).astype(BF16)
        ke_t = (k * e_suf).T.astype(BF16)
        e_pre_t = e_pre.T
        for j in range(n_seq):
            s0 = s0_ref[j, h]
            mine = row_seq == j
            o = o + jnp.where(mine, _dot(qe, s0.astype(BF16)), 0.0)
            last = j * seq + seq - 1
            s_ref[j, h] = (e_pre_t[:, last:last + 1] * s0
                           + _dot(ke_t, jnp.where(mine, v, 0.0).astype(BF16)))
        o_ref[:, cols] = _head_out(o, sg_ref[:, cols], gn_ref[:, cols])


def _hgrn_sample(lf, q, k, v, sg, gn, s0, batch, seq):
    n_seq = SLAB // seq
    mall, masks, nlev = _hier_consts(seq, n_seq)
    spec = pl.BlockSpec((SLAB, W_A), lambda i: (i, 0))
    sspec = pl.BlockSpec((n_seq, H_A, DK, DV), lambda i: (i, 0, 0, 0))
    return pl.pallas_call(
        functools.partial(_hgrn_sample_kernel, nlev=nlev, seq=seq, n_seq=n_seq),
        grid=(batch // n_seq,),
        in_specs=[spec] * 5 + [pl.BlockSpec((1, W_A), lambda i: (0, 0)), sspec,
                               pl.BlockSpec(mall.shape, lambda i: (0, 0)),
                               pl.BlockSpec(masks.shape, lambda i: (0, 0, 0))],
        out_specs=[spec, sspec],
        out_shape=[jax.ShapeDtypeStruct((batch * seq, W_A), BF16),
                   jax.ShapeDtypeStruct((batch, H_A, DK, DV), F32)],
        compiler_params=_params(("parallel",)),
        name="hgrn_sample",
    )(lf, q, k, v, sg, gn, s0, mall, masks)


def _ln_silu(c, lg, lb):
    mu = jnp.mean(c, axis=-1, keepdims=True)
    d = c - mu
    var = jnp.mean(d * d, axis=-1, keepdims=True)
    y = d * lax.rsqrt(var + EPS) * lg + lb
    return (y * _sigmoid(y)).astype(BF16)


def _conv_prompt_kernel(u_ref, cw_ref, cb_ref, lg_ref, lb_ref, c_o, nb_o, win, cs, *, lt, rows):
    t = pl.program_id(1)

    @pl.when(t == 0)
    def _():
        win[0:HIST_PAD, :] = jnp.zeros((HIST_PAD, C_B), F32)

    win[HIST_PAD:HIST_PAD + lt, :] = u_ref[...]
    cb = cb_ref[...]
    lg = lg_ref[...]
    lb = lb_ref[...]

    lead = HIST_PAD - HIST

    def sub(i, carry):
        r0 = pl.multiple_of(i * rows, rows)
        for cblk in range(C_B // 128):
            cols = slice(cblk * 128, (cblk + 1) * 128)
            wnd = win[pl.ds(r0, rows + HIST_PAD), cols]
            acc = jnp.zeros((rows, 128), F32) + cb[:, cols]
            for s in range(8):
                offs = [o for o in range(s, HIST_PAD + 1, 8) if lead <= o < lead + CONV_W]
                ws = wnd[s:s + offs[-1] - s + rows]
                for o in offs:
                    acc = acc + ws[o - s:o - s + rows] * cw_ref[o - lead:o - lead + 1, cols]
            cs[:, cols] = acc
        c_o[pl.ds(r0, rows), :] = _ln_silu(cs[...], lg, lb)
        return carry

    lax.fori_loop(0, lt // rows, sub, 0)

    @pl.when(t == pl.num_programs(1) - 1)
    def _():
        nb_o[0] = win[lt + HIST_PAD - HIST:lt + HIST_PAD, :]

    win[0:HIST_PAD, :] = win[lt:lt + HIST_PAD, :]


def _conv_prompt(u, cw, cb, lg, lb, batch, seq, lt=512, rows=64):
    nt = seq // lt
    vec = pl.BlockSpec((1, C_B), lambda b, t: (0, 0))
    return pl.pallas_call(
        functools.partial(_conv_prompt_kernel, lt=lt, rows=rows),
        grid=(batch, nt),
        in_specs=[pl.BlockSpec((lt, C_B), lambda b, t: (b * nt + t, 0)),
                  pl.BlockSpec((CONV_W, C_B), lambda b, t: (0, 0)), vec, vec, vec],
        out_specs=[pl.BlockSpec((lt, C_B), lambda b, t: (b * nt + t, 0)),
                   pl.BlockSpec((1, HIST, C_B), lambda b, t: (b, 0, 0))],
        out_shape=[jax.ShapeDtypeStruct((batch * seq, C_B), BF16),
                   jax.ShapeDtypeStruct((batch, HIST, C_B), F32)],
        scratch_shapes=[pltpu.VMEM((lt + HIST_PAD, C_B), F32), pltpu.VMEM((rows, C_B), F32)],
        compiler_params=_params(("parallel", "arbitrary")),
        name="conv_prompt",
    )(u, cw, cb, lg, lb)


def _conv_sample_kernel(u_ref, buf_ref, cw_ref, cb_ref, lg_ref, lb_ref, c_o, nb_o, win, *, seq, nb):
    cb = cb_ref[...]
    lg = lg_ref[...]
    lb = lb_ref[...]
    for j in range(nb):
        rows = slice(j * seq, (j + 1) * seq)
        u = u_ref[rows, :]
        win[0:HIST, :] = buf_ref[j]
        win[HIST:HIST + seq, :] = u
        acc = jnp.zeros((seq, C_B), F32) + cb
        for w in range(CONV_W):
            acc = acc + win[w:w + seq, :] * cw_ref[w:w + 1, :]
        c_o[rows, :] = _ln_silu(acc, lg, lb)
        nb_o[j] = win[seq:seq + HIST, :]


def _conv_sample(u, buf, cw, cb, lg, lb, batch, seq, nb=8):
    vec = pl.BlockSpec((1, C_B), lambda i: (0, 0))
    return pl.pallas_call(
        functools.partial(_conv_sample_kernel, seq=seq, nb=nb),
        grid=(batch // nb,),
        in_specs=[pl.BlockSpec((nb * seq, C_B), lambda i: (i, 0)),
                  pl.BlockSpec((nb, HIST, C_B), lambda i: (i, 0, 0)),
                  pl.BlockSpec((CONV_W, C_B), lambda i: (0, 0)), vec, vec, vec],
        out_specs=[pl.BlockSpec((nb * seq, C_B), lambda i: (i, 0)),
                   pl.BlockSpec((nb, HIST, C_B), lambda i: (i, 0, 0))],
        out_shape=[jax.ShapeDtypeStruct((batch * seq, C_B), BF16),
                   jax.ShapeDtypeStruct((batch, HIST, C_B), F32)],
        scratch_shapes=[pltpu.VMEM((HIST + seq + 2, C_B), F32)],
        compiler_params=_params(("parallel",)),
        name="conv_sample",
    )(u, buf, cw, cb, lg, lb)


def _outproj_kernel(o_ref, c_ref, x_ref, w_ref, g2_ref, h_o, hn_o):
    m = _dot(o_ref[...], w_ref[0:W_A, :]) + _dot(c_ref[...], w_ref[W_A:D_MODEL, :])
    h = x_ref[...] + m
    h_o[...] = h
    ms = jnp.mean(h * h, axis=-1, keepdims=True)
    hn_o[...] = (h * lax.rsqrt(ms + EPS) * g2_ref[...]).astype(BF16)


def _out_proj(o, c, x2d, w_out_bf, g2, tm=512):
    T = x2d.shape[0]
    return pl.pallas_call(
        _outproj_kernel,
        grid=(T // tm,),
        in_specs=[pl.BlockSpec((tm, W_A), lambda i: (i, 0)),
                  pl.BlockSpec((tm, C_B), lambda i: (i, 0)),
                  pl.BlockSpec((tm, D_MODEL), lambda i: (i, 0)),
                  pl.BlockSpec((D_MODEL, D_MODEL), lambda i: (0, 0)),
                  pl.BlockSpec((1, D_MODEL), lambda i: (0, 0))],
        out_specs=[pl.BlockSpec((tm, D_MODEL), lambda i: (i, 0))] * 2,
        out_shape=[jax.ShapeDtypeStruct((T, D_MODEL), F32),
                   jax.ShapeDtypeStruct((T, D_MODEL), BF16)],
        compiler_params=_params(("parallel",)),
        name="out_proj",
    )(o, c, x2d, w_out_bf, g2)


def _mlp_kernel(hn_ref, h_ref, wu_ref, wd_ref, gf_ref, y_o, acc, *, final_norm):
    f = pl.program_id(1)

    @pl.when(f == 0)
    def _():
        acc[...] = h_ref[...]

    a = jnp.maximum(_dot(hn_ref[...], wu_ref[...]), 0.0)
    acc[...] += _dot((a * a).astype(BF16), wd_ref[...])

    @pl.when(f == pl.num_programs(1) - 1)
    def _():
        y = acc[...]
        if final_norm:
            ms = jnp.mean(y * y, axis=-1, keepdims=True)
            y = y * lax.rsqrt(ms + EPS) * gf_ref[...]
        y_o[...] = y


def _mlp(hn, h, w_up_bf, w_down_bf, gf, final_norm, tm=512, tf=1024):
    T = h.shape[0]
    return pl.pallas_call(
        functools.partial(_mlp_kernel, final_norm=final_norm),
        grid=(T // tm, D_FF // tf),
        in_specs=[pl.BlockSpec((tm, D_MODEL), lambda i, f: (i, 0)),
                  pl.BlockSpec((tm, D_MODEL), lambda i, f: (i, 0)),
                  pl.BlockSpec((D_MODEL, tf), lambda i, f: (0, f)),
                  pl.BlockSpec((tf, D_MODEL), lambda i, f: (f, 0)),
                  pl.BlockSpec((1, D_MODEL), lambda i, f: (0, 0))],
        out_specs=pl.BlockSpec((tm, D_MODEL), lambda i, f: (i, 0)),
        out_shape=jax.ShapeDtypeStruct((T, D_MODEL), F32),
        scratch_shapes=[pltpu.VMEM((tm, D_MODEL), F32)],
        compiler_params=_params(("parallel", "arbitrary")),
        name="mlp",
    )(hn, h, w_up_bf, w_down_bf, gf)


def kernel(x_prompt, x_sample, state_hgrn, state_conv, lower_bounds, norm1_g, w_in, hgrn_norm_g,
           conv_w, conv_b, conv_ln_g, conv_ln_b, w_out, norm2_g, w_up, w_down, final_norm_g):
    depth = w_in.shape[0]
    bp, lp, _ = x_prompt.shape
    bs, ls, _ = x_sample.shape
    xp = x_prompt.reshape(bp * lp, D_MODEL)
    xs = x_sample.reshape(bs * ls, D_MODEL)
    gf = final_norm_g.reshape(1, D_MODEL)
    hp, cp, hs, cs = [], [], [], []
    for l in range(depth):
        w_in_bf = w_in[l].astype(BF16)
        w_out_bf = w_out[l].astype(BF16)
        w_up_bf = w_up[l].astype(BF16)
        w_down_bf = w_down[l].astype(BF16)
        g1 = norm1_g[l].reshape(1, D_MODEL)
        g2 = norm2_g[l].reshape(1, D_MODEL)
        gn = hgrn_norm_g[l].reshape(1, W_A)
        cb = conv_b[l].reshape(1, C_B)
        lg = conv_ln_g[l].reshape(1, C_B)
        lb = conv_ln_b[l].reshape(1, C_B)
        last = l == depth - 1

        def tail(x2d, o, c):
            h, hn = _out_proj(o, c, x2d, w_out_bf, g2)
            return _mlp(hn, h, w_up_bf, w_down_bf, gf, final_norm=last)

        q, lf, k, v, sg, u = _in_proj(xp, g1, lower_bounds, w_in_bf, l)
        o, s_new = _hgrn_prompt(lf, q, k, v, sg, gn, bp, lp)
        c, buf_new = _conv_prompt(u, conv_w[l], cb, lg, lb, bp, lp)
        xp = tail(xp, o, c)
        hp.append(s_new)
        cp.append(buf_new)

        q, lf, k, v, sg, u = _in_proj(xs, g1, lower_bounds, w_in_bf, l)
        o, s_new = _hgrn_sample(lf, q, k, v, sg, gn, state_hgrn[l], bs, ls)
        c, buf_new = _conv_sample(u, state_conv[l], conv_w[l], cb, lg, lb, bs, ls)
        xs = tail(xs, o, c)
        hs.append(s_new)
        cs.append(buf_new)

    return (xp.reshape(bp, lp, D_MODEL), xs.reshape(bs, ls, D_MODEL),
            jnp.stack(hp, axis=0), jnp.stack(cp, axis=0), jnp.stack(hs, axis=0), jnp.stack(cs, axis=0))
```
